```python
import math
import jax, jax.numpy as jnp
from jax import lax
import numpy as np

D_MODEL = 1024
BATCH = 1
SEQ = 16384
DEPTH = 4

HEAD_DIM = 64
D_MIX = D_MODEL
N_HEADS_TOTAL = D_MIX // HEAD_DIM
CONV_CH = D_MIX // 4
CONV_K = 3
MLA_HEADS = (D_MIX // 2) // HEAD_DIM
QK_NOPE = HEAD_DIM
QK_ROPE = HEAD_DIM // 2
QK_HEAD = QK_NOPE + QK_ROPE
V_HEAD = HEAD_DIM
Q_LORA = D_MODEL // 4
KV_LORA = D_MODEL // 8
ROPE_THETA = 10000.0
Q_BLOCK = 128
LRU_W = D_MIX // 4
LRU_BLOCKS = LRU_W // HEAD_DIM
LRU_BW = LRU_W // LRU_BLOCKS
LRU_CONV_K = 4
LRU_C = 8.0
IN_SIZES = (CONV_CH, CONV_CH, CONV_CH, Q_LORA, KV_LORA, QK_ROPE, LRU_W, LRU_W)
IN_COLS = 3 * CONV_CH + Q_LORA + KV_LORA + QK_ROPE + 2 * LRU_W
N_EXPERTS = 16
N_GROUPS = 4
EXPERTS_PER_GROUP = N_EXPERTS // N_GROUPS
TOP_K = 2
D_FF_EXPERT = D_MODEL // 4
EPS = 1e-6
NEG_INF = -1e30

kernel_name = 'hymba_style_conv_mla_rglru_grouped_moe_adaln'


def _rmsnorm(x, g):
    xf = x.astype(jnp.float32)
    y = xf * lax.rsqrt(jnp.mean(xf * xf, axis=-1, keepdims=True) + EPS)
    return (y * g.astype(jnp.float32)).astype(x.dtype)


def _causal_dwconv(x, w):
    k = w.shape[0]
    return lax.conv_general_dilated(
        x, w[:, None, :].astype(x.dtype), window_strides=(1,), padding=[(k - 1, 0)],
        dimension_numbers=('NWC', 'WIO', 'NWC'), feature_group_count=x.shape[-1])


def _rope(x, cos, sin):
    half = x.shape[-1] // 2
    x1 = x[..., :half].astype(jnp.float32)
    x2 = x[..., half:].astype(jnp.float32)
    return jnp.concatenate([x1 * cos - x2 * sin, x2 * cos + x1 * sin], axis=-1).astype(x.dtype)


def _causal_block_attention(q, k, v, positions):
    B, S, H, Dq = q.shape
    nb = S // Q_BLOCK
    scale = 1.0 / math.sqrt(Dq)
    qb = jnp.moveaxis(q.reshape(B, nb, Q_BLOCK, H, Dq), 1, 0)
    pb = jnp.moveaxis(positions.reshape(B, nb, Q_BLOCK), 1, 0)

    def one_block(args):
        q_blk, p_blk = args
        s = jnp.einsum('bqhd,bkhd->bhqk', q_blk, k).astype(jnp.float32) * scale
        mask = positions[:, None, None, :] <= p_blk[:, None, :, None]
        s = jnp.where(mask, s, NEG_INF)
        p = jax.nn.softmax(s, axis=-1).astype(v.dtype)
        return jnp.einsum('bhqk,bkhd->bqhd', p, v)

    o = lax.map(one_block, (qb, pb))
    return jnp.moveaxis(o, 0, 1).reshape(B, S, H * v.shape[-1])


def _mla(c_q, c_kv, k_rope, positions, g_q_lora, g_kv_lora, w_q_up, w_kv_up, g_q_head, g_k_head):
    B, S, _ = c_q.shape
    q = (_rmsnorm(c_q, g_q_lora) @ w_q_up).reshape(B, S, MLA_HEADS, QK_HEAD)
    kv = (_rmsnorm(c_kv, g_kv_lora) @ w_kv_up).reshape(B, S, MLA_HEADS, QK_NOPE + V_HEAD)
    k_nope, v = kv[..., :QK_NOPE], kv[..., QK_NOPE:]
    k = jnp.concatenate(
        [k_nope, jnp.broadcast_to(k_rope[:, :, None, :], (B, S, MLA_HEADS, QK_ROPE))], axis=-1)
    q = _rmsnorm(q, g_q_head)
    k = _rmsnorm(k, g_k_head)
    inv_freq = ROPE_THETA ** (-jnp.arange(0, QK_ROPE, 2, dtype=jnp.float32) / QK_ROPE)
    ang = positions.astype(jnp.float32)[..., None] * inv_freq
    cos = jnp.cos(ang)[:, :, None, :]
    sin = jnp.sin(ang)[:, :, None, :]
    q = jnp.concatenate([q[..., :QK_NOPE], _rope(q[..., QK_NOPE:], cos, sin)], axis=-1)
    k = jnp.concatenate([k[..., :QK_NOPE], _rope(k[..., QK_NOPE:], cos, sin)], axis=-1)
    return _causal_block_attention(q, k, v, positions)


def _rg_lru(x, w_r, b_r, w_i, b_i, lam):
    B, S, W = x.shape
    xb = x.reshape(B, S, LRU_BLOCKS, LRU_BW)
    r = jax.nn.sigmoid(jnp.einsum('bsnc,ncd->bsnd', xb, w_r).reshape(B, S, W) + b_r)
    i = jax.nn.sigmoid(jnp.einsum('bsnc,ncd->bsnd', xb, w_i).reshape(B, S, W) + b_i)
    log_a = -LRU_C * r.astype(jnp.float32) * jax.nn.softplus(-lam.astype(jnp.float32))
    a = jnp.exp(log_a)
    u = jnp.sqrt(-jnp.expm1(2.0 * log_a)) * (i * x).astype(jnp.float32)

    def combine(left, right):
        a1, b1 = left
        a2, b2 = right
        return a1 * a2, a2 * b1 + b2

    _, h = lax.associative_scan(combine, (a, u), axis=1)
    return h.astype(x.dtype)


def _shared_grouped_router(h, w_router, b_router):
    B, S, _ = h.shape
    scores = jax.nn.sigmoid((h @ w_router).astype(jnp.float32))
    sel = scores + b_router.astype(jnp.float32)
    grp = lax.top_k(sel.reshape(B, S, N_GROUPS, EXPERTS_PER_GROUP), 2)[0].sum(-1)
    best = jnp.argmax(grp, axis=-1)
    in_group = (jnp.arange(N_EXPERTS) // EXPERTS_PER_GROUP) == best[..., None]
    _, idx = lax.top_k(jnp.where(in_group, sel, -jnp.inf), TOP_K)
    w = jnp.take_along_axis(scores, idx, axis=-1)
    w = w / jnp.sum(w, axis=-1, keepdims=True)
    return jnp.sum(jax.nn.one_hot(idx, N_EXPERTS, dtype=jnp.float32) * w[..., None], axis=-2)


def _moe(h, gates, w_g, w_u, w_d):
    a = jnp.einsum('bsd,edf->bsef', h, w_g)
    u = jnp.einsum('bsd,edf->bsef', h, w_u)
    act = jax.nn.silu(a) * u * gates[..., None].astype(h.dtype)
    return jnp.einsum('bsef,efd->bsd', act, w_d)


def setup_inputs(seed: int = 0) -> dict:
    key = jax.random.key(seed)
    ks = iter(jax.random.split(key, 40))
    f32 = jnp.float32

    def nrm(shape, scale):
        return jax.random.normal(next(ks), shape, f32) * scale

    def gain(shape):
        return 1.0 + 0.1 * jax.random.normal(next(ks), shape, f32)

    L = DEPTH
    x = jax.random.normal(next(ks), (BATCH, SEQ, D_MODEL), f32)
    c = jax.random.normal(next(ks), (BATCH, D_MODEL), f32)
    positions = jnp.broadcast_to(jnp.arange(SEQ, dtype=jnp.int32), (BATCH, SEQ))
    u = jax.random.uniform(next(ks), (L, LRU_W), f32, minval=0.9, maxval=0.999)
    p = u ** (1.0 / LRU_C)
    lru_lambda = jnp.log(p) - jnp.log1p(-p)
    return {
        'x': x,
        'c': c,
        'positions': positions,
        'w_mod': nrm((L, D_MODEL, 6 * D_MODEL), 0.5 * D_MODEL ** -0.5),
        'b_mod': nrm((L, 6 * D_MODEL), 0.02),
        'g_mix_norm': gain((L, D_MODEL)),
        'g_ffn_norm': gain((L, D_MODEL)),
        'w_in': nrm((L, D_MODEL, IN_COLS), D_MODEL ** -0.5),
        'conv_w': nrm((L, CONV_K, CONV_CH), CONV_K ** -0.5),
        'g_q_lora': gain((L, Q_LORA)),
        'g_kv_lora': gain((L, KV_LORA)),
        'w_q_up': nrm((L, Q_LORA, MLA_HEADS * QK_HEAD), Q_LORA ** -0.5),
        'w_kv_up': nrm((L, KV_LORA, MLA_HEADS * (QK_NOPE + V_HEAD)), KV_LORA ** -0.5),
        'g_q_head': gain((L, QK_HEAD)),
        'g_k_head': gain((L, QK_HEAD)),
        'lru_conv_w': nrm((L, LRU_CONV_K, LRU_W), LRU_CONV_K ** -0.5),
        'lru_conv_b': nrm((L, LRU_W), 0.02),
        'w_rgate': nrm((L, LRU_BLOCKS, LRU_BW, LRU_BW), LRU_BW ** -0.5),
        'b_rgate': nrm((L, LRU_W), 0.02),
        'w_igate': nrm((L, LRU_BLOCKS, LRU_BW, LRU_BW), LRU_BW ** -0.5),
        'b_igate': nrm((L, LRU_W), 0.02),
        'lru_lambda': lru_lambda,
        'g_head_out': gain((L, D_MIX)),
        'w_out': nrm((L, D_MIX, D_MODEL), D_MIX ** -0.5),
        'w_router': nrm((D_MODEL, N_EXPERTS), D_MODEL ** -0.5),
        'b_router': nrm((N_EXPERTS,), 0.01),
        'w_exp_gate': nrm((L, N_EXPERTS, D_MODEL, D_FF_EXPERT), D_MODEL ** -0.5),
        'w_exp_up': nrm((L, N_EXPERTS, D_MODEL, D_FF_EXPERT), D_MODEL ** -0.5),
        'w_exp_down': nrm((L, N_EXPERTS, D_FF_EXPERT, D_MODEL), D_FF_EXPERT ** -0.5),
    }


def reference(x, c, positions, w_mod, b_mod, g_mix_norm, g_ffn_norm, w_in, conv_w,
              g_q_lora, g_kv_lora, w_q_up, w_kv_up, g_q_head, g_k_head,
              lru_conv_w, lru_conv_b, w_rgate, b_rgate, w_igate, b_igate, lru_lambda,
              g_head_out, w_out, w_router, b_router, w_exp_gate, w_exp_up, w_exp_down):
    B, S, _ = x.shape
    split_at = np.cumsum(IN_SIZES)[:-1].tolist()
    c_act = jax.nn.silu(c)
    for l in range(DEPTH):
        mod = (c_act @ w_mod[l] + b_mod[l])[:, None, :]
        sh1, sc1, ga1, sh2, sc2, ga2 = jnp.split(mod, 6, axis=-1)

        h = _rmsnorm(x, g_mix_norm[l]) * (1.0 + sc1) + sh1
        z = h @ w_in[l]
        xa, gb, gc, c_q, c_kv, k_rope, x_lru, g_lru = jnp.split(z, split_at, axis=-1)

        y_conv = gb * _causal_dwconv(gc * xa, conv_w[l])

        y_mla = _mla(c_q, c_kv, k_rope, positions, g_q_lora[l], g_kv_lora[l],
                     w_q_up[l], w_kv_up[l], g_q_head[l], g_k_head[l])

        xr = _causal_dwconv(x_lru, lru_conv_w[l]) + lru_conv_b[l]
        y_lru = _rg_lru(xr, w_rgate[l], b_rgate[l], w_igate[l], b_igate[l], lru_lambda[l])
        y_lru = y_lru * jax.nn.gelu(g_lru)

        y = jnp.concatenate([y_conv, y_mla, y_lru], axis=-1).reshape(B, S, N_HEADS_TOTAL, HEAD_DIM)
        y = _rmsnorm(y, g_head_out[l].reshape(N_HEADS_TOTAL, HEAD_DIM)).reshape(B, S, D_MIX)
        x = x + ga1 * (y @ w_out[l])

        h = _rmsnorm(x, g_ffn_norm[l]) * (1.0 + sc2) + sh2
        gates = _shared_grouped_router(h, w_router, b_router)
        x = x + ga2 * _moe(h, gates, w_exp_gate[l], w_exp_up[l], w_exp_down[l])
    return x
```

```python
import functools
import math

import numpy as np
import jax
import jax.numpy as jnp
from jax import lax
from jax.experimental import pallas as pl
from jax.experimental.pallas import tpu as pltpu

F32 = jnp.float32
BF16 = jnp.bfloat16

D_MODEL = 1024
DEPTH = 4
HEAD_DIM = 64
CONV_CH = 256
MLA_HEADS = 8
QK_NOPE = 64
QK_ROPE = 32
QK_HEAD = 96
V_HEAD = 64
Q_LORA = 256
KV_LORA = 128
ROPE_THETA = 10000.0
LRU_W = 256
LRU_BLOCKS = 4
LRU_BW = 64
LRU_C = 8.0
IN_COLS = 1696
N_EXPERTS = 16
N_GROUPS = 4
EXPERTS_PER_GROUP = 4
D_FF = 256
EPS = 1e-6
NEG_INF = -1e30

LANES = 128
SUBLANES = 8
HP = LANES
ZC_COLS = 5 * CONV_CH
IN_EXT_COLS = ZC_COLS + Q_LORA + KV_LORA + 2 * HP
VMEM_LIMIT = 56 * 1024 * 1024

TM_IN = 512
TS_SEQ = 512
TQ_ATT = 512
TK_ATT = 512
TM_OUT = 512
TR_ROPE = 2048

Q_SCALE = (1.0 / math.sqrt(QK_HEAD)) * math.log2(math.e)


def _cparams(sem):
    return pltpu.CompilerParams(dimension_semantics=sem, vmem_limit_bytes=VMEM_LIMIT)


def _mod_kernel(c_ref, w_ref, b_ref, o_ref):
    c = c_ref[...]
    ca = c * jax.nn.sigmoid(c)
    ca8 = jnp.broadcast_to(ca, (SUBLANES, D_MODEL))
    o_ref[...] = jnp.dot(ca8, w_ref[...], preferred_element_type=F32) + b_ref[...]


def _mod_call(c, w_mod, b_mod):
    L = w_mod.shape[0]
    nj = 6
    return pl.pallas_call(
        _mod_kernel,
        grid=(L, nj),
        in_specs=[
            pl.BlockSpec((1, D_MODEL), lambda l, j: (0, 0)),
            pl.BlockSpec((None, D_MODEL, D_MODEL), lambda l, j: (l, 0, j)),
            pl.BlockSpec((None, 1, D_MODEL), lambda l, j: (l, 0, j)),
        ],
        out_specs=pl.BlockSpec((None, SUBLANES, D_MODEL), lambda l, j: (l, 0, j)),
        out_shape=jax.ShapeDtypeStruct((L, SUBLANES, 6 * D_MODEL), F32),
        compiler_params=_cparams(("arbitrary", "arbitrary")),
        name="adaln_mod",
    )(c, w_mod, b_mod.reshape(L, 1, 6 * D_MODEL))


def _rope_kernel(pos_ref, expo_ref, cs_ref, sn_ref):
    pos = pos_ref[...].astype(F32)
    inv_freq = jnp.power(ROPE_THETA, expo_ref[...])
    ang = pos * inv_freq
    lane = lax.broadcasted_iota(jnp.int32, ang.shape, 1)
    c = jnp.cos(ang)
    s = jnp.sin(ang)
    in_rope = (lane >= QK_NOPE) & (lane < QK_HEAD)
    cs_ref[...] = jnp.where(in_rope, c, jnp.where(lane < QK_NOPE, 1.0, 0.0))
    sn_ref[...] = jnp.where(in_rope, jnp.where(lane < QK_NOPE + QK_ROPE // 2, -s, s), 0.0)


def _rope_call(positions):
    S = positions.shape[1]
    tr = min(TR_ROPE, S)
    half = QK_ROPE // 2
    expo = np.zeros((1, LANES), np.float32)
    e = -np.arange(0, QK_ROPE, 2, dtype=np.float32) / QK_ROPE
    expo[0, QK_NOPE:QK_NOPE + half] = e
    expo[0, QK_NOPE + half:QK_HEAD] = e
    return pl.pallas_call(
        _rope_kernel,
        grid=(S // tr,),
        in_specs=[
            pl.BlockSpec((tr, 1), lambda i: (i, 0)),
            pl.BlockSpec((1, LANES), lambda i: (0, 0)),
        ],
        out_specs=[pl.BlockSpec((tr, LANES), lambda i: (i, 0))] * 2,
        out_shape=[jax.ShapeDtypeStruct((S, LANES), F32)] * 2,
        compiler_params=_cparams(("arbitrary",)),
        name="rope_tables",
    )(positions.reshape(S, 1), jnp.asarray(expo))


def _in_kernel(x_ref, mod_ref, g_ref, win_ref, gq_ref, gkv_ref, wq_ref, wkv_ref, rg_ref,
               ones_ref, cs_ref, sn_ref, zc_ref, q_ref, k_ref, v_ref):
    x = x_ref[...]
    sh = mod_ref[0:1, 0:D_MODEL]
    sc = mod_ref[0:1, D_MODEL:2 * D_MODEL]
    ms = jnp.mean(x * x, axis=-1, keepdims=True)
    h = x * lax.rsqrt(ms + EPS) * g_ref[...]
    h = h * (1.0 + sc) + sh
    z = jnp.dot(h.astype(BF16), win_ref[...], preferred_element_type=F32)
    zc_ref[...] = z[:, :ZC_COLS]

    o = ZC_COLS
    cq = z[:, o:o + Q_LORA]
    o += Q_LORA
    ckv = z[:, o:o + KV_LORA]
    o += KV_LORA
    kr = z[:, o:o + HP]
    krs = z[:, o + HP:o + 2 * HP]

    cqn = cq * lax.rsqrt(jnp.mean(cq * cq, axis=-1, keepdims=True) + EPS) * gq_ref[...]
    qq = jnp.dot(cqn.astype(BF16), wq_ref[...], preferred_element_type=F32)
    ckvn = ckv * lax.rsqrt(jnp.mean(ckv * ckv, axis=-1, keepdims=True) + EPS) * gkv_ref[...]
    kv = jnp.dot(ckvn.astype(BF16), wkv_ref[...], preferred_element_type=F32)

    cs = cs_ref[...]
    sn = sn_ref[...]
    a_q = cs * rg_ref[0:1, :]
    b_q = sn * rg_ref[1:2, :]
    a_k = cs * rg_ref[2:3, :]
    b_k = sn * rg_ref[3:4, :]
    krs_b = krs * b_k
    nh = MLA_HEADS
    for hd in range(nh):
        qh = qq[:, hd * HP:(hd + 1) * HP]
        qs = qq[:, (nh + hd) * HP:(nh + hd + 1) * HP]
        rq = lax.rsqrt(jnp.sum(qh * qh, axis=-1, keepdims=True) * (1.0 / QK_HEAD) + EPS) * Q_SCALE
        q_ref[hd] = ((qh * a_q + qs * b_q) * rq).astype(BF16)
        kh = kv[:, hd * HP:(hd + 1) * HP] + kr
        rk = lax.rsqrt(jnp.sum(kh * kh, axis=-1, keepdims=True) * (1.0 / QK_HEAD) + EPS)
        k_ref[hd] = ((kh * a_k + krs_b) * rk).astype(BF16)
        vh = kv[:, (nh + hd) * HP:(nh + hd + 1) * HP] + ones_ref[:, hd * HP:(hd + 1) * HP]
        v_ref[hd] = vh.astype(BF16)


def _in_call(l, x2, mod, g_mix, win_ext, g_q_lora, g_kv_lora, wq_ext, wkv_ext, rope_gain,
             ones_row, cs, sn):
    S = x2.shape[0]
    tm = min(TM_IN, S)
    nh = MLA_HEADS
    row = lambda i: (i, 0)
    lay = lambda i: (l, 0, 0)
    hs = lambda i: (0, i, 0)
    return pl.pallas_call(
        _in_kernel,
        grid=(S // tm,),
        in_specs=[
            pl.BlockSpec((tm, D_MODEL), row),
            pl.BlockSpec((None, SUBLANES, 6 * D_MODEL), lay),
            pl.BlockSpec((None, 1, D_MODEL), lay),
            pl.BlockSpec((None, D_MODEL, IN_EXT_COLS), lay),
            pl.BlockSpec((None, 1, Q_LORA), lay),
            pl.BlockSpec((None, 1, KV_LORA), lay),
            pl.BlockSpec((None, Q_LORA, 2 * nh * HP), lay),
            pl.BlockSpec((None, KV_LORA, 2 * nh * HP), lay),
            pl.BlockSpec((None, 4, HP), lay),
            pl.BlockSpec((1, nh * HP), lambda i: (0, 0)),
            pl.BlockSpec((tm, HP), row),
            pl.BlockSpec((tm, HP), row),
        ],
        out_specs=[
            pl.BlockSpec((tm, ZC_COLS), row),
            pl.BlockSpec((nh, tm, HP), hs),
            pl.BlockSpec((nh, tm, HP), hs),
            pl.BlockSpec((nh, tm, HP), hs),
        ],
        out_shape=[
            jax.ShapeDtypeStruct((S, ZC_COLS), F32),
            jax.ShapeDtypeStruct((nh, S, HP), BF16),
            jax.ShapeDtypeStruct((nh, S, HP), BF16),
            jax.ShapeDtypeStruct((nh, S, HP), BF16),
        ],
        compiler_params=_cparams(("arbitrary",)),
        name="in_proj_mla_prep",
    )(x2, mod, g_mix, win_ext, g_q_lora, g_kv_lora, wq_ext, wkv_ext, rope_gain, ones_row, cs, sn)


def _class_scan(a_ref, b_ref, c, n):
    acs, bcs = [], []
    for r in range(SUBLANES):
        a = a_ref[c, pl.ds(r, n, stride=SUBLANES), :]
        b = b_ref[c, pl.ds(r, n, stride=SUBLANES), :]
        if r == 0:
            acs.append(a)
            bcs.append(b)
        else:
            bcs.append(a * bcs[-1] + b)
            acs.append(a * acs[-1])
    return acs, bcs


def _seq_kernel(zc_ref, cw_ref, lw_ref, lb_ref, wr_ref, br_ref, wi_ref, bi_ref, lam_ref,
                yc_ref, yl_ref,
                p_scr, xl_scr, a_scr, u_scr, h_scr, a2_scr, b2_scr, c1_scr, hcar_scr):
    ts = zc_ref.shape[0]
    n1 = ts // SUBLANES
    n2 = n1 // SUBLANES
    halo = SUBLANES
    i = pl.program_id(0)

    @pl.when(i == 0)
    def _():
        p_scr[0:halo, :] = jnp.zeros((halo, CONV_CH), F32)
        xl_scr[0:halo, :] = jnp.zeros((halo, LRU_W), F32)
        hcar_scr[...] = jnp.zeros_like(hcar_scr)

    xa = zc_ref[:, 0:CONV_CH]
    gb = zc_ref[:, CONV_CH:2 * CONV_CH]
    gc = zc_ref[:, 2 * CONV_CH:3 * CONV_CH]
    p = gc * xa
    p_scr[halo:halo + ts, :] = p
    conv = (cw_ref[0:1, :] * p_scr[halo - 2:halo - 2 + ts, :]
            + cw_ref[1:2, :] * p_scr[halo - 1:halo - 1 + ts, :]
            + cw_ref[2:3, :] * p)
    yc_ref[...] = (gb * conv).astype(yc_ref.dtype)
    p_scr[0:halo, :] = p[ts - halo:ts, :]

    xl = zc_ref[:, 3 * CONV_CH:3 * CONV_CH + LRU_W]
    xl_scr[halo:halo + ts, :] = xl
    xr = (lw_ref[0:1, :] * xl_scr[halo - 3:halo - 3 + ts, :]
          + lw_ref[1:2, :] * xl_scr[halo - 2:halo - 2 + ts, :]
          + lw_ref[2:3, :] * xl_scr[halo - 1:halo - 1 + ts, :]
          + lw_ref[3:4, :] * xl + lb_ref[...])
    xl_scr[0:halo, :] = xl[ts - halo:ts, :]

    xrb = xr.astype(BF16)
    rg = jax.nn.sigmoid(jnp.dot(xrb, wr_ref[...], preferred_element_type=F32) + br_ref[...])
    ig = jax.nn.sigmoid(jnp.dot(xrb, wi_ref[...], preferred_element_type=F32) + bi_ref[...])
    nl = -lam_ref[...]
    softplus = jnp.maximum(nl, 0.0) + jnp.log(1.0 + jnp.exp(-jnp.abs(nl)))
    log_a = (-LRU_C) * rg * softplus
    a = jnp.exp(log_a)
    u = jnp.sqrt(1.0 - a * a) * (ig * xr)
    nt = LRU_W // LANES
    for c in range(nt):
        a_scr[c] = a[:, c * LANES:(c + 1) * LANES]
        u_scr[c] = u[:, c * LANES:(c + 1) * LANES]

    for c in range(nt):
        ac1, bc1 = _class_scan(a_scr, u_scr, c, n1)
        a2_scr[c] = ac1[-1]
        b2_scr[c] = bc1[-1]
        ac2, bc2 = _class_scan(a2_scr, b2_scr, c, n2)
        a3 = ac2[-1]
        b3 = bc2[-1]
        e = hcar_scr[c]
        cin2_rows = []
        for j in range(n2):
            cin2_rows.append(e)
            e = a3[j:j + 1, :] * e + b3[j:j + 1, :]
        hcar_scr[c] = e
        cin2 = jnp.concatenate(cin2_rows, axis=0)
        cin1 = cin2
        for r in range(SUBLANES):
            c1_scr[c, pl.ds(r, n2, stride=SUBLANES), :] = cin1
            cin1 = bc2[r] + ac2[r] * cin2
        cin = c1_scr[c]
        for r in range(SUBLANES):
            h_scr[c, pl.ds(r, n1, stride=SUBLANES), :] = bc1[r] + ac1[r] * cin
    hs = jnp.concatenate([h_scr[c] for c in range(nt)], axis=-1)

    g = zc_ref[:, 4 * CONV_CH:4 * CONV_CH + LRU_W]
    gelu = 0.5 * g * (1.0 + jnp.tanh(math.sqrt(2.0 / math.pi) * (g + 0.044715 * (g * g * g))))
    yl_ref[...] = (hs * gelu).astype(yl_ref.dtype)


def _seq_call(l, zc, conv_w, lru_conv_w, lru_conv_b, wr_bd, b_r, wi_bd, b_i, lam):
    S = zc.shape[0]
    ts = min(TS_SEQ, S)
    assert ts % (SUBLANES * SUBLANES) == 0 and ts // (SUBLANES * SUBLANES) == SUBLANES
    row = lambda i: (i, 0)
    lay = lambda i: (l, 0, 0)
    n1 = ts // SUBLANES
    nt = LRU_W // LANES
    return pl.pallas_call(
        _seq_kernel,
        grid=(S // ts,),
        in_specs=[
            pl.BlockSpec((ts, ZC_COLS), row),
            pl.BlockSpec((None, 3, CONV_CH), lay),
            pl.BlockSpec((None, 4, LRU_W), lay),
            pl.BlockSpec((None, 1, LRU_W), lay),
            pl.BlockSpec((None, LRU_W, LRU_W), lay),
            pl.BlockSpec((None, 1, LRU_W), lay),
            pl.BlockSpec((None, LRU_W, LRU_W), lay),
            pl.BlockSpec((None, 1, LRU_W), lay),
            pl.BlockSpec((None, 1, LRU_W), lay),
        ],
        out_specs=[pl.BlockSpec((ts, CONV_CH), row), pl.BlockSpec((ts, LRU_W), row)],
        out_shape=[jax.ShapeDtypeStruct((S, CONV_CH), BF16), jax.ShapeDtypeStruct((S, LRU_W), BF16)],
        scratch_shapes=[
            pltpu.VMEM((SUBLANES + ts, CONV_CH), F32),
            pltpu.VMEM((SUBLANES + ts, LRU_W), F32),
            pltpu.VMEM((nt, ts, LANES), F32),
            pltpu.VMEM((nt, ts, LANES), F32),
            pltpu.VMEM((nt, ts, LANES), F32),
            pltpu.VMEM((nt, n1, LANES), F32),
            pltpu.VMEM((nt, n1, LANES), F32),
            pltpu.VMEM((nt, n1, LANES), F32),
            pltpu.VMEM((nt, 1, LANES), F32),
        ],
        compiler_params=_cparams(("arbitrary",)),
        name="seq_mixers",
    )(zc, conv_w, lru_conv_w, lru_conv_b, wr_bd, b_r, wi_bd, b_i, lam)


def _attn_kernel(q_ref, k_ref, v_ref, o_ref):
    tq = q_ref.shape[1]
    tk = tq
    i = pl.program_id(1)
    dn = (((1,), (1,)), ((), ()))
    accs = []
    for hh in range(2):
        q = q_ref[hh]
        d0 = pl.multiple_of(i * tq, tq)
        s = lax.dot_general(q, k_ref[hh, pl.ds(d0, tq), :], dn, preferred_element_type=F32)
        rowi = lax.broadcasted_iota(jnp.int32, s.shape, 0)
        coli = lax.broadcasted_iota(jnp.int32, s.shape, 1)
        s = jnp.where(coli <= rowi, s, NEG_INF)
        m = jnp.max(s, axis=-1, keepdims=True)
        p = jnp.exp2(s - m)
        acc = jnp.dot(p.astype(BF16), v_ref[hh, pl.ds(d0, tq), :], preferred_element_type=F32)

        def body(j, carry, hh=hh, q=q):
            m, acc = carry
            k0 = pl.multiple_of(j * tk, tk)
            s = lax.dot_general(q, k_ref[hh, pl.ds(k0, tk), :], dn, preferred_element_type=F32)
            m_new = jnp.maximum(m, jnp.max(s, axis=-1, keepdims=True))
            alpha = jnp.exp2(m - m_new)
            p = jnp.exp2(s - m_new)
            acc = acc * alpha + jnp.dot(p.astype(BF16), v_ref[hh, pl.ds(k0, tk), :],
                                        preferred_element_type=F32)
            return m_new, acc

        m, acc = lax.fori_loop(0, i, body, (m, acc))
        accs.append(acc)
    l0 = accs[0][:, V_HEAD:V_HEAD + 1]
    l1 = accs[1][:, 0:1]
    lane = lax.broadcasted_iota(jnp.int32, accs[0].shape, 1)
    o = jnp.where(lane < V_HEAD, accs[0] / l0, accs[1] / l1)
    o_ref[...] = o.astype(o_ref.dtype)


def _attn_call(q, k, v):
    nh, S, _ = q.shape
    tq = min(TQ_ATT, S)
    return pl.pallas_call(
        _attn_kernel,
        grid=(nh // 2, S // tq),
        in_specs=[
            pl.BlockSpec((2, tq, HP), lambda p, i: (p, i, 0)),
            pl.BlockSpec((2, S, HP), lambda p, i: (p, 0, 0)),
            pl.BlockSpec((2, S, HP), lambda p, i: (p, 0, 0)),
        ],
        out_specs=pl.BlockSpec((tq, 2 * V_HEAD), lambda p, i: (i, p)),
        out_shape=jax.ShapeDtypeStruct((S, nh * V_HEAD), BF16),
        compiler_params=_cparams(("arbitrary", "arbitrary")),
        name="mla_attention",
    )(q, k, v)


def _route(logits_t, bias_col):
    scores = jax.nn.sigmoid(logits_t)
    sel = scores + bias_col
    sc = [scores[e:e + 1, :] for e in range(N_EXPERTS)]
    sl = [sel[e:e + 1, :] for e in range(N_EXPERTS)]
    grp = []
    for g in range(N_GROUPS):
        v = sl[g * EXPERTS_PER_GROUP:(g + 1) * EXPERTS_PER_GROUP]
        best = None
        for a in range(EXPERTS_PER_GROUP):
            for b in range(a + 1, EXPERTS_PER_GROUP):
                pair = v[a] + v[b]
                best = pair if best is None else jnp.maximum(best, pair)
        grp.append(best)
    gates = []
    for g in range(N_GROUPS):
        is_best = None
        for g2 in range(N_GROUPS):
            if g2 == g:
                continue
            c = (grp[g] > grp[g2]) if g2 < g else (grp[g] >= grp[g2])
            is_best = c if is_best is None else jnp.logical_and(is_best, c)
        v = sl[g * EXPERTS_PER_GROUP:(g + 1) * EXPERTS_PER_GROUP]
        w = sc[g * EXPERTS_PER_GROUP:(g + 1) * EXPERTS_PER_GROUP]
        picked = []
        for a in range(EXPERTS_PER_GROUP):
            rank = jnp.zeros_like(v[a])
            for b in range(EXPERTS_PER_GROUP):
                if b == a:
                    continue
                ahead = (v[b] >= v[a]) if b < a else (v[b] > v[a])
                rank = rank + jnp.where(ahead, 1.0, 0.0)
            picked.append(jnp.where(jnp.logical_and(is_best, rank < 2.0), w[a], 0.0))
        gates.extend(picked)
    total = gates[0]
    for e in range(1, N_EXPERTS):
        total = total + gates[e]
    inv = 1.0 / total
    return [gt * inv for gt in gates]


def _out_moe_kernel(x_ref, yc_ref, ym_ref, yl_ref, mod_ref, gho_ref, e_ref, et_ref, wout_ref,
                    gffn_ref, wr_ref, br_ref, wgu_ref, wd_ref, o_ref, h2_scr, gate_scr, gt_scr):
    g = pl.program_id(1)
    ga2 = mod_ref[0:1, 5 * D_MODEL:6 * D_MODEL]

    @pl.when(g == 0)
    def _():
        ga1 = mod_ref[0:1, 2 * D_MODEL:3 * D_MODEL]
        sh2 = mod_ref[0:1, 3 * D_MODEL:4 * D_MODEL]
        sc2 = mod_ref[0:1, 4 * D_MODEL:5 * D_MODEL]
        y = jnp.concatenate([yc_ref[...], ym_ref[...], yl_ref[...]], axis=-1).astype(F32)
        ss = jnp.dot((y * y).astype(BF16), e_ref[...], preferred_element_type=F32)
        rinv = lax.rsqrt(ss * (1.0 / HEAD_DIM) + EPS)
        r_hi = rinv.astype(BF16)
        r_lo = (rinv - r_hi.astype(F32)).astype(BF16)
        rb = (jnp.dot(r_hi, et_ref[...], preferred_element_type=F32)
              + jnp.dot(r_lo, et_ref[...], preferred_element_type=F32))
        yn = y * rb * gho_ref[...]
        x1 = x_ref[...] + ga1 * jnp.dot(yn.astype(BF16), wout_ref[...], preferred_element_type=F32)
        o_ref[...] = x1
        ms = jnp.mean(x1 * x1, axis=-1, keepdims=True)
        h2 = x1 * lax.rsqrt(ms + EPS) * gffn_ref[...]
        h2 = h2 * (1.0 + sc2) + sh2
        h2_scr[...] = h2.astype(BF16)
        logits = jnp.dot(h2, wr_ref[...], preferred_element_type=F32,
                         precision=lax.Precision.HIGHEST)
        gates = _route(logits.T[0:N_EXPERTS, :], br_ref[0:N_EXPERTS, :])
        gt_scr[...] = jnp.zeros_like(gt_scr)
        for gg in range(N_GROUPS):
            for e in range(EXPERTS_PER_GROUP):
                gt_scr[gg, e:e + 1, :] = gates[gg * EXPERTS_PER_GROUP + e]
            gate_scr[gg] = gt_scr[gg].T

    h2b = h2_scr[...]
    gates_g = gate_scr[g]
    acts = []
    for e in range(EXPERTS_PER_GROUP):
        au = jnp.dot(h2b, wgu_ref[e], preferred_element_type=F32)
        a = au[:, :D_FF]
        u = au[:, D_FF:]
        act = (a * jax.nn.sigmoid(a)) * u * gates_g[:, e:e + 1]
        acts.append(act.astype(BF16))
    act = jnp.concatenate(acts, axis=-1)
    o_ref[...] += ga2 * jnp.dot(act, wd_ref[...], preferred_element_type=F32)


def _out_moe_call(l, x2, yc, ym, yl, mod, g_head_out, e_mat, et_mat, w_out, g_ffn, w_router_pad,
                  b_router_col, wgu, wd):
    S = x2.shape[0]
    tm = min(TM_OUT, S)
    row = lambda i, g: (i, 0)
    lay = lambda i, g: (l, 0, 0)
    fix = lambda i, g: (0, 0)
    epg = EXPERTS_PER_GROUP
    return pl.pallas_call(
        _out_moe_kernel,
        grid=(S // tm, N_GROUPS),
        in_specs=[
            pl.BlockSpec((tm, D_MODEL), row),
            pl.BlockSpec((tm, CONV_CH), row),
            pl.BlockSpec((tm, MLA_HEADS * V_HEAD), row),
            pl.BlockSpec((tm, LRU_W), row),
            pl.BlockSpec((None, SUBLANES, 6 * D_MODEL), lay),
            pl.BlockSpec((None, 1, D_MODEL), lay),
            pl.BlockSpec((D_MODEL, LANES), fix),
            pl.BlockSpec((LANES, D_MODEL), fix),
            pl.BlockSpec((None, D_MODEL, D_MODEL), lay),
            pl.BlockSpec((None, 1, D_MODEL), lay),
            pl.BlockSpec((D_MODEL, LANES), fix),
            pl.BlockSpec((LANES, 1), fix),
            pl.BlockSpec((None, epg, D_MODEL, 2 * D_FF), lambda i, g: (l, g, 0, 0)),
            pl.BlockSpec((None, epg * D_FF, D_MODEL), lambda i, g: (l, g, 0)),
        ],
        out_specs=pl.BlockSpec((tm, D_MODEL), row),
        out_shape=jax.ShapeDtypeStruct((S, D_MODEL), F32),
        scratch_shapes=[
            pltpu.VMEM((tm, D_MODEL), BF16),
            pltpu.VMEM((N_GROUPS, tm, LANES), F32),
            pltpu.VMEM((N_GROUPS, LANES, tm), F32),
        ],
        compiler_params=_cparams(("arbitrary", "arbitrary")),
        name="out_proj_moe",
    )(x2, yc, ym, yl, mod, g_head_out, e_mat, et_mat, w_out, g_ffn, w_router_pad, b_router_col,
      wgu, wd)


def _take_cols(w, idx, n_src):
    pad = jnp.zeros(w.shape[:-1] + (1,), w.dtype)
    return jnp.take(jnp.concatenate([w, pad], axis=-1), jnp.asarray(idx, jnp.int32), axis=-1)


def _in_proj_cols():
    zero = IN_COLS
    half = QK_ROPE // 2
    kro = 3 * CONV_CH + Q_LORA + KV_LORA
    lru0 = kro + QK_ROPE
    idx = list(range(0, 3 * CONV_CH))
    idx += list(range(lru0, lru0 + 2 * LRU_W))
    idx += list(range(3 * CONV_CH, kro))
    idx += [zero] * QK_NOPE + list(range(kro, kro + QK_ROPE)) + [zero] * (HP - QK_HEAD)
    idx += ([zero] * QK_NOPE + list(range(kro + half, kro + QK_ROPE)) + list(range(kro, kro + half))
            + [zero] * (HP - QK_HEAD))
    assert len(idx) == IN_EXT_COLS
    return idx


def _q_up_cols():
    zero = MLA_HEADS * QK_HEAD
    half = QK_ROPE // 2
    idx = []
    for h in range(MLA_HEADS):
        idx += list(range(h * QK_HEAD, (h + 1) * QK_HEAD)) + [zero] * (HP - QK_HEAD)
    for h in range(MLA_HEADS):
        r0 = h * QK_HEAD + QK_NOPE
        idx += ([zero] * QK_NOPE + list(range(r0 + half, r0 + QK_ROPE)) + list(range(r0, r0 + half))
                + [zero] * (HP - QK_HEAD))
    return idx


def _kv_up_cols():
    zero = MLA_HEADS * (QK_NOPE + V_HEAD)
    per = QK_NOPE + V_HEAD
    idx = []
    for h in range(MLA_HEADS):
        idx += list(range(h * per, h * per + QK_NOPE)) + [zero] * (HP - QK_NOPE)
    for h in range(MLA_HEADS):
        vcols = list(range(h * per + QK_NOPE, (h + 1) * per))
        idx += (vcols + [zero] * (HP - V_HEAD)) if h % 2 == 0 else ([zero] * (HP - V_HEAD) + vcols)
    return idx


def _ones_row():
    o = np.zeros((1, MLA_HEADS * HP), np.float32)
    for h in range(MLA_HEADS):
        o[0, h * HP + (V_HEAD if h % 2 == 0 else 0)] = 1.0
    return jnp.asarray(o)


def _rope_gain_rows(g):
    L = g.shape[0]
    half = QK_ROPE // 2
    z_tail = jnp.zeros((L, HP - QK_HEAD), g.dtype)
    z_nope = jnp.zeros((L, QK_NOPE), g.dtype)
    g1 = jnp.concatenate([g, z_tail], axis=-1)
    g2 = jnp.concatenate([z_nope, g[:, QK_NOPE + half:], g[:, QK_NOPE:QK_NOPE + half], z_tail], axis=-1)
    return jnp.stack([g1, g2], axis=1)


def _block_diag(w):
    L, nb, bw, _ = w.shape
    eye = jnp.eye(nb, dtype=w.dtype)
    return jnp.einsum("lncd,nm->lncmd", w, eye).reshape(L, nb * bw, nb * bw)


def kernel(x, c, positions, w_mod, b_mod, g_mix_norm, g_ffn_norm, w_in, conv_w, g_q_lora, g_kv_lora, w_q_up, w_kv_up, g_q_head, g_k_head, lru_conv_w, lru_conv_b, w_rgate, b_rgate, w_igate, b_igate, lru_lambda, g_head_out, w_out, w_router, b_router, w_exp_gate, w_exp_up, w_exp_down):
    B, S, D = x.shape
    assert B == 1 and D == D_MODEL
    L = w_mod.shape[0]

    win_ext = _take_cols(w_in, _in_proj_cols(), IN_COLS).astype(BF16)
    wq_ext = _take_cols(w_q_up, _q_up_cols(), MLA_HEADS * QK_HEAD).astype(BF16)
    wkv_ext = _take_cols(w_kv_up, _kv_up_cols(), MLA_HEADS * (QK_NOPE + V_HEAD)).astype(BF16)
    rope_gain = jnp.concatenate([_rope_gain_rows(g_q_head), _rope_gain_rows(g_k_head)], axis=1)
    ones_row = _ones_row()
    wr_bd = _block_diag(w_rgate).astype(BF16)
    wi_bd = _block_diag(w_igate).astype(BF16)
    r3 = lambda a: a.reshape(L, 1, a.shape[-1])
    head_of_col = np.arange(D_MODEL) // HEAD_DIM
    e_np = (head_of_col[:, None] == np.arange(LANES)[None, :]).astype(np.float32)
    e_mat = jnp.asarray(e_np, BF16)
    et_mat = jnp.asarray(e_np.T, BF16)
    w_out_b = w_out.astype(BF16)
    w_router_pad = jnp.pad(w_router, ((0, 0), (0, LANES - N_EXPERTS)))
    b_router_col = jnp.pad(b_router, (0, LANES - N_EXPERTS)).reshape(LANES, 1)
    wgu = jnp.concatenate([w_exp_gate, w_exp_up], axis=-1).astype(BF16)
    wd = w_exp_down.astype(BF16).reshape(L, N_EXPERTS * D_FF, D_MODEL)

    mod = _mod_call(c, w_mod, b_mod)
    cs, sn = _rope_call(positions)

    x2 = x.reshape(S, D)
    for l in range(L):
        zc, q, k, v = _in_call(l, x2, mod, r3(g_mix_norm), win_ext, r3(g_q_lora), r3(g_kv_lora),
                               wq_ext, wkv_ext, rope_gain, ones_row, cs, sn)
        yc, yl = _seq_call(l, zc, conv_w, lru_conv_w, r3(lru_conv_b), wr_bd, r3(b_rgate), wi_bd,
                           r3(b_igate), r3(lru_lambda))
        ym = _attn_call(q, k, v)
        x2 = _out_moe_call(l, x2, yc, ym, yl, mod, r3(g_head_out), e_mat, et_mat, w_out_b,
                           r3(g_ffn_norm), w_router_pad, b_router_col, wgu, wd)
    return x2.reshape(B, S, D)
```

```python
import functools
import math

import numpy as np
import jax
import jax.numpy as jnp
from jax import lax
from jax.experimental import pallas as pl
from jax.experimental.pallas import tpu as pltpu

F32 = jnp.float32
BF16 = jnp.bfloat16

D_MODEL = 1024
DEPTH = 4
HEAD_DIM = 64
CONV_CH = 256
MLA_HEADS = 8
QK_NOPE = 64
QK_ROPE = 32
QK_HEAD = 96
V_HEAD = 64
Q_LORA = 256
KV_LORA = 128
ROPE_THETA = 10000.0
LRU_W = 256
LRU_BLOCKS = 4
LRU_BW = 64
LRU_C = 8.0
IN_COLS = 1696
N_EXPERTS = 16
N_GROUPS = 4
EXPERTS_PER_GROUP = 4
D_FF = 256
EPS = 1e-6
NEG_INF = -1e30

LANES = 128
SUBLANES = 8
HP = LANES
ZC_COLS = 5 * CONV_CH
IN_EXT_COLS = ZC_COLS + Q_LORA + KV_LORA + 2 * HP
VMEM_LIMIT = 56 * 1024 * 1024

TM_IN = 512
TS_SEQ = 512
TQ_ATT = 1024
TM_OUT = 512
TR_ROPE = 2048

Q_SCALE = (1.0 / math.sqrt(QK_HEAD)) * math.log2(math.e)


def _cparams(sem):
    return pltpu.CompilerParams(dimension_semantics=sem, vmem_limit_bytes=VMEM_LIMIT)


def _mod_kernel(c_ref, w_ref, b_ref, o_ref):
    c = c_ref[...]
    ca = c * jax.nn.sigmoid(c)
    ca8 = jnp.broadcast_to(ca, (SUBLANES, D_MODEL))
    o_ref[...] = jnp.dot(ca8, w_ref[...], preferred_element_type=F32) + b_ref[...]


def _mod_call(c, w_mod, b_mod):
    L = w_mod.shape[0]
    nj = 6
    return pl.pallas_call(
        _mod_kernel,
        grid=(L, nj),
        in_specs=[
            pl.BlockSpec((1, D_MODEL), lambda l, j: (0, 0)),
            pl.BlockSpec((None, D_MODEL, D_MODEL), lambda l, j: (l, 0, j)),
            pl.BlockSpec((None, 1, D_MODEL), lambda l, j: (l, 0, j)),
        ],
        out_specs=pl.BlockSpec((None, SUBLANES, D_MODEL), lambda l, j: (l, 0, j)),
        out_shape=jax.ShapeDtypeStruct((L, SUBLANES, 6 * D_MODEL), F32),
        compiler_params=_cparams(("arbitrary", "arbitrary")),
        name="adaln_mod",
    )(c, w_mod, b_mod.reshape(L, 1, 6 * D_MODEL))


def _rope_kernel(pos_ref, expo_ref, cs_ref, sn_ref):
    pos = pos_ref[...].astype(F32)
    inv_freq = jnp.power(ROPE_THETA, expo_ref[...])
    ang = pos * inv_freq
    lane = lax.broadcasted_iota(jnp.int32, ang.shape, 1)
    c = jnp.cos(ang)
    s = jnp.sin(ang)
    in_rope = (lane >= QK_NOPE) & (lane < QK_HEAD)
    cs_ref[...] = jnp.where(in_rope, c, jnp.where(lane < QK_NOPE, 1.0, 0.0))
    sn_ref[...] = jnp.where(in_rope, jnp.where(lane < QK_NOPE + QK_ROPE // 2, -s, s), 0.0)


def _rope_call(positions):
    S = positions.shape[1]
    tr = min(TR_ROPE, S)
    half = QK_ROPE // 2
    expo = np.zeros((1, LANES), np.float32)
    e = -np.arange(0, QK_ROPE, 2, dtype=np.float32) / QK_ROPE
    expo[0, QK_NOPE:QK_NOPE + half] = e
    expo[0, QK_NOPE + half:QK_HEAD] = e
    return pl.pallas_call(
        _rope_kernel,
        grid=(S // tr,),
        in_specs=[
            pl.BlockSpec((tr, 1), lambda i: (i, 0)),
            pl.BlockSpec((1, LANES), lambda i: (0, 0)),
        ],
        out_specs=[pl.BlockSpec((tr, LANES), lambda i: (i, 0))] * 2,
        out_shape=[jax.ShapeDtypeStruct((S, LANES), F32)] * 2,
        compiler_params=_cparams(("arbitrary",)),
        name="rope_tables",
    )(positions.reshape(S, 1), jnp.asarray(expo))


def _in_kernel(x_ref, mod_ref, g_ref, win_ref, gq_ref, gkv_ref, wq_ref, wkv_ref, rg_ref,
               ones_ref, cs_ref, sn_ref, zc_ref, q_ref, k_ref, v_ref):
    x = x_ref[...]
    sh = mod_ref[0:1, 0:D_MODEL]
    sc = mod_ref[0:1, D_MODEL:2 * D_MODEL]
    ms = jnp.mean(x * x, axis=-1, keepdims=True)
    h = x * lax.rsqrt(ms + EPS) * g_ref[...]
    h = h * (1.0 + sc) + sh
    z = jnp.dot(h.astype(BF16), win_ref[...], preferred_element_type=F32)
    zc_ref[...] = z[:, :ZC_COLS]

    o = ZC_COLS
    cq = z[:, o:o + Q_LORA]
    o += Q_LORA
    ckv = z[:, o:o + KV_LORA]
    o += KV_LORA
    kr = z[:, o:o + HP]
    krs = z[:, o + HP:o + 2 * HP]

    cqn = cq * lax.rsqrt(jnp.mean(cq * cq, axis=-1, keepdims=True) + EPS) * gq_ref[...]
    qq = jnp.dot(cqn.astype(BF16), wq_ref[...], preferred_element_type=F32)
    ckvn = ckv * lax.rsqrt(jnp.mean(ckv * ckv, axis=-1, keepdims=True) + EPS) * gkv_ref[...]
    kv = jnp.dot(ckvn.astype(BF16), wkv_ref[...], preferred_element_type=F32)

    cs = cs_ref[...]
    sn = sn_ref[...]
    a_q = cs * rg_ref[0:1, :]
    b_q = sn * rg_ref[1:2, :]
    a_k = cs * rg_ref[2:3, :]
    b_k = sn * rg_ref[3:4, :]
    krs_b = krs * b_k
    nh = MLA_HEADS
    for hd in range(nh):
        qh = qq[:, hd * HP:(hd + 1) * HP]
        qs = qq[:, (nh + hd) * HP:(nh + hd + 1) * HP]
        rq = lax.rsqrt(jnp.sum(qh * qh, axis=-1, keepdims=True) * (1.0 / QK_HEAD) + EPS) * Q_SCALE
        q_ref[hd] = ((qh * a_q + qs * b_q) * rq).astype(BF16)
        kh = kv[:, hd * HP:(hd + 1) * HP] + kr
        rk = lax.rsqrt(jnp.sum(kh * kh, axis=-1, keepdims=True) * (1.0 / QK_HEAD) + EPS)
        k_ref[hd] = ((kh * a_k + krs_b) * rk).astype(BF16)
        vh = kv[:, (nh + hd) * HP:(nh + hd + 1) * HP] + ones_ref[:, hd * HP:(hd + 1) * HP]
        v_ref[hd] = vh.astype(BF16)


def _in_call(l, x2, mod, g_mix, win_ext, g_q_lora, g_kv_lora, wq_ext, wkv_ext, rope_gain,
             ones_row, cs, sn):
    S = x2.shape[0]
    tm = min(TM_IN, S)
    nh = MLA_HEADS
    row = lambda i: (i, 0)
    lay = lambda i: (l, 0, 0)
    hs = lambda i: (0, i, 0)
    return pl.pallas_call(
        _in_kernel,
        grid=(S // tm,),
        in_specs=[
            pl.BlockSpec((tm, D_MODEL), row),
            pl.BlockSpec((None, SUBLANES, 6 * D_MODEL), lay),
            pl.BlockSpec((None, 1, D_MODEL), lay),
            pl.BlockSpec((None, D_MODEL, IN_EXT_COLS), lay),
            pl.BlockSpec((None, 1, Q_LORA), lay),
            pl.BlockSpec((None, 1, KV_LORA), lay),
            pl.BlockSpec((None, Q_LORA, 2 * nh * HP), lay),
            pl.BlockSpec((None, KV_LORA, 2 * nh * HP), lay),
            pl.BlockSpec((None, 4, HP), lay),
            pl.BlockSpec((1, nh * HP), lambda i: (0, 0)),
            pl.BlockSpec((tm, HP), row),
            pl.BlockSpec((tm, HP), row),
        ],
        out_specs=[
            pl.BlockSpec((tm, ZC_COLS), row),
            pl.BlockSpec((nh, tm, HP), hs),
            pl.BlockSpec((nh, tm, HP), hs),
            pl.BlockSpec((nh, tm, HP), hs),
        ],
        out_shape=[
            jax.ShapeDtypeStruct((S, ZC_COLS), F32),
            jax.ShapeDtypeStruct((nh, S, HP), BF16),
            jax.ShapeDtypeStruct((nh, S, HP), BF16),
            jax.ShapeDtypeStruct((nh, S, HP), BF16),
        ],
        compiler_params=_cparams(("arbitrary",)),
        name="in_proj_mla_prep",
    )(x2, mod, g_mix, win_ext, g_q_lora, g_kv_lora, wq_ext, wkv_ext, rope_gain, ones_row, cs, sn)


def _class_scan(a_ref, b_ref, c, n):
    acs, bcs = [], []
    for r in range(SUBLANES):
        a = a_ref[c, pl.ds(r, n, stride=SUBLANES), :]
        b = b_ref[c, pl.ds(r, n, stride=SUBLANES), :]
        if r == 0:
            acs.append(a)
            bcs.append(b)
        else:
            bcs.append(a * bcs[-1] + b)
            acs.append(a * acs[-1])
    return acs, bcs


def _seq_kernel(zc_ref, cw_ref, lw_ref, lb_ref, wr_ref, br_ref, wi_ref, bi_ref, lam_ref,
                yc_ref, yl_ref,
                p_scr, xl_scr, a_scr, u_scr, h_scr, a2_scr, b2_scr, c1_scr, hcar_scr):
    ts = zc_ref.shape[0]
    n1 = ts // SUBLANES
    n2 = n1 // SUBLANES
    halo = SUBLANES
    i = pl.program_id(0)

    @pl.when(i == 0)
    def _():
        p_scr[0:halo, :] = jnp.zeros((halo, CONV_CH), F32)
        xl_scr[0:halo, :] = jnp.zeros((halo, LRU_W), F32)
        hcar_scr[...] = jnp.zeros_like(hcar_scr)

    xa = zc_ref[:, 0:CONV_CH]
    gb = zc_ref[:, CONV_CH:2 * CONV_CH]
    gc = zc_ref[:, 2 * CONV_CH:3 * CONV_CH]
    p = gc * xa
    p_scr[halo:halo + ts, :] = p
    conv = (cw_ref[0:1, :] * p_scr[halo - 2:halo - 2 + ts, :]
            + cw_ref[1:2, :] * p_scr[halo - 1:halo - 1 + ts, :]
            + cw_ref[2:3, :] * p)
    yc_ref[...] = (gb * conv).astype(yc_ref.dtype)
    p_scr[0:halo, :] = p[ts - halo:ts, :]

    xl = zc_ref[:, 3 * CONV_CH:3 * CONV_CH + LRU_W]
    xl_scr[halo:halo + ts, :] = xl
    xr = (lw_ref[0:1, :] * xl_scr[halo - 3:halo - 3 + ts, :]
          + lw_ref[1:2, :] * xl_scr[halo - 2:halo - 2 + ts, :]
          + lw_ref[2:3, :] * xl_scr[halo - 1:halo - 1 + ts, :]
          + lw_ref[3:4, :] * xl + lb_ref[...])
    xl_scr[0:halo, :] = xl[ts - halo:ts, :]

    xrb = xr.astype(BF16)
    rg = jax.nn.sigmoid(jnp.dot(xrb, wr_ref[...], preferred_element_type=F32) + br_ref[...])
    ig = jax.nn.sigmoid(jnp.dot(xrb, wi_ref[...], preferred_element_type=F32) + bi_ref[...])
    nl = -lam_ref[...]
    softplus = jnp.maximum(nl, 0.0) + jnp.log(1.0 + jnp.exp(-jnp.abs(nl)))
    log_a = (-LRU_C) * rg * softplus
    a = jnp.exp(log_a)
    u = jnp.sqrt(1.0 - a * a) * (ig * xr)
    nt = LRU_W // LANES
    for c in range(nt):
        a_scr[c] = a[:, c * LANES:(c + 1) * LANES]
        u_scr[c] = u[:, c * LANES:(c + 1) * LANES]

    for c in range(nt):
        ac1, bc1 = _class_scan(a_scr, u_scr, c, n1)
        a2_scr[c] = ac1[-1]
        b2_scr[c] = bc1[-1]
        ac2, bc2 = _class_scan(a2_scr, b2_scr, c, n2)
        a3 = ac2[-1]
        b3 = bc2[-1]
        e = hcar_scr[c]
        cin2_rows = []
        for j in range(n2):
            cin2_rows.append(e)
            e = a3[j:j + 1, :] * e + b3[j:j + 1, :]
        hcar_scr[c] = e
        cin2 = jnp.concatenate(cin2_rows, axis=0)
        cin1 = cin2
        for r in range(SUBLANES):
            c1_scr[c, pl.ds(r, n2, stride=SUBLANES), :] = cin1
            cin1 = bc2[r] + ac2[r] * cin2
        cin = c1_scr[c]
        for r in range(SUBLANES):
            h_scr[c, pl.ds(r, n1, stride=SUBLANES), :] = bc1[r] + ac1[r] * cin
    hs = jnp.concatenate([h_scr[c] for c in range(nt)], axis=-1)

    g = zc_ref[:, 4 * CONV_CH:4 * CONV_CH + LRU_W]
    gelu = 0.5 * g * (1.0 + jnp.tanh(math.sqrt(2.0 / math.pi) * (g + 0.044715 * (g * g * g))))
    yl_ref[...] = (hs * gelu).astype(yl_ref.dtype)


def _seq_call(l, zc, conv_w, lru_conv_w, lru_conv_b, wr_bd, b_r, wi_bd, b_i, lam):
    S = zc.shape[0]
    ts = min(TS_SEQ, S)
    assert ts % (SUBLANES * SUBLANES) == 0 and ts // (SUBLANES * SUBLANES) == SUBLANES
    row = lambda i: (i, 0)
    lay = lambda i: (l, 0, 0)
    n1 = ts // SUBLANES
    nt = LRU_W // LANES
    return pl.pallas_call(
        _seq_kernel,
        grid=(S // ts,),
        in_specs=[
            pl.BlockSpec((ts, ZC_COLS), row),
            pl.BlockSpec((None, 3, CONV_CH), lay),
            pl.BlockSpec((None, 4, LRU_W), lay),
            pl.BlockSpec((None, 1, LRU_W), lay),
            pl.BlockSpec((None, LRU_W, LRU_W), lay),
            pl.BlockSpec((None, 1, LRU_W), lay),
            pl.BlockSpec((None, LRU_W, LRU_W), lay),
            pl.BlockSpec((None, 1, LRU_W), lay),
            pl.BlockSpec((None, 1, LRU_W), lay),
        ],
        out_specs=[pl.BlockSpec((ts, CONV_CH), row), pl.BlockSpec((ts, LRU_W), row)],
        out_shape=[jax.ShapeDtypeStruct((S, CONV_CH), BF16), jax.ShapeDtypeStruct((S, LRU_W), BF16)],
        scratch_shapes=[
            pltpu.VMEM((SUBLANES + ts, CONV_CH), F32),
            pltpu.VMEM((SUBLANES + ts, LRU_W), F32),
            pltpu.VMEM((nt, ts, LANES), F32),
            pltpu.VMEM((nt, ts, LANES), F32),
            pltpu.VMEM((nt, ts, LANES), F32),
            pltpu.VMEM((nt, n1, LANES), F32),
            pltpu.VMEM((nt, n1, LANES), F32),
            pltpu.VMEM((nt, n1, LANES), F32),
            pltpu.VMEM((nt, 1, LANES), F32),
        ],
        compiler_params=_cparams(("arbitrary",)),
        name="seq_mixers",
    )(zc, conv_w, lru_conv_w, lru_conv_b, wr_bd, b_r, wi_bd, b_i, lam)


def _attn_kernel(q_ref, k_ref, v_ref, o_ref, s00, s01, s10, s11, p00, p01, p10, p11,
                 m0, m1, a0, a1):
    tq = q_ref.shape[1]
    tk = tq // 2
    s_scr = ((s00, s01), (s10, s11))
    p_scr = ((p00, p01), (p10, p11))
    m_scr = (m0, m1)
    acc_scr = (a0, a1)
    i = pl.program_id(1)
    dn = (((1,), (1,)), ((), ()))
    heads = range(2)

    def scores(b0, slot):
        for hh in heads:
            s_scr[slot][hh][...] = lax.dot_general(
                q_ref[hh], k_ref[hh, pl.ds(b0, tk), :], dn, preferred_element_type=F32)

    def pv(b0, slot):
        return [jnp.dot(p_scr[slot][hh][...], v_ref[hh, pl.ds(b0, tk), :],
                        preferred_element_type=F32) for hh in heads]

    def softmax(slot, diag, first):
        alphas = []
        nc = tk // LANES
        for hh in heads:
            cols = []
            for c in range(nc):
                s = s_scr[slot][hh][:, c * LANES:(c + 1) * LANES]
                if diag is not None:
                    rowi = lax.broadcasted_iota(jnp.int32, s.shape, 0)
                    coli = lax.broadcasted_iota(jnp.int32, s.shape, 1) + (diag * tk + c * LANES)
                    s = jnp.where(coli <= rowi, s, NEG_INF)
                cols.append(s)
            mx = cols[0]
            for c in range(1, nc):
                mx = jnp.maximum(mx, cols[c])
            m_cur = jnp.max(mx, axis=-1, keepdims=True)
            if first:
                m_new = jnp.broadcast_to(m_cur, (tq, LANES))
                alphas.append(None)
            else:
                m_old = m_scr[hh][...]
                m_new = jnp.maximum(m_old, m_cur)
                alphas.append(jnp.exp2(m_old - m_new))
            m_scr[hh][...] = m_new
            for c in range(nc):
                p_scr[slot][hh][:, c * LANES:(c + 1) * LANES] = jnp.exp2((cols[c] - m_new).astype(BF16))
        return alphas

    def step(b0, slot, diag, issue_next, first_pv):
        if issue_next:
            scores(b0 + tk, 1 - slot)
        pvs = pv(b0 - tk, 1 - slot)
        alphas = softmax(slot, diag, False)
        for hh in heads:
            prev = pvs[hh] if first_pv else acc_scr[hh][...] + pvs[hh]
            acc_scr[hh][...] = prev * alphas[hh]

    def finish(b_last):
        pvs = pv(b_last, 1)
        a0 = acc_scr[0][...] + pvs[0]
        a1 = acc_scr[1][...] + pvs[1]
        l0 = a0[:, V_HEAD:V_HEAD + 1]
        l1 = a1[:, 0:1]
        lane = lax.broadcasted_iota(jnp.int32, a0.shape, 1)
        o_ref[...] = jnp.where(lane < V_HEAD, a0 / l0, a1 / l1).astype(o_ref.dtype)

    @pl.when(i == 0)
    def _():
        scores(0, 0)
        scores(tk, 1)
        softmax(0, 0, True)
        step(tk, 1, 1, False, True)
        finish(tk)

    @pl.when(i > 0)
    def _():
        scores(0, 0)
        scores(tk, 1)
        softmax(0, None, True)
        step(tk, 1, None, True, True)

        def body(t, carry):
            b0 = pl.multiple_of(t * tq, tq)
            step(b0, 0, None, True, False)
            step(b0 + tk, 1, None, True, False)
            return carry

        lax.fori_loop(1, i, body, 0)
        d0 = pl.multiple_of(i * tq, tq)
        step(d0, 0, 0, True, False)
        step(d0 + tk, 1, 1, False, False)
        finish(d0 + tk)


def _attn_call(q, k, v):
    nh, S, _ = q.shape
    tq = min(TQ_ATT, S)
    return pl.pallas_call(
        _attn_kernel,
        grid=(nh // 2, S // tq),
        in_specs=[
            pl.BlockSpec((2, tq, HP), lambda p, i: (p, i, 0)),
            pl.BlockSpec((2, S, HP), lambda p, i: (p, 0, 0), pipeline_mode=pl.Buffered(1)),
            pl.BlockSpec((2, S, HP), lambda p, i: (p, 0, 0), pipeline_mode=pl.Buffered(1)),
        ],
        out_specs=pl.BlockSpec((tq, 2 * V_HEAD), lambda p, i: (i, p)),
        out_shape=jax.ShapeDtypeStruct((S, nh * V_HEAD), BF16),
        scratch_shapes=([pltpu.VMEM((tq, tq // 2), F32)] * 4 + [pltpu.VMEM((tq, tq // 2), BF16)] * 4
                        + [pltpu.VMEM((tq, LANES), F32)] * 2 + [pltpu.VMEM((tq, HP), F32)] * 2),
        compiler_params=_cparams(("arbitrary", "arbitrary")),
        name="mla_attention",
    )(q, k, v)


def _route(logits_t, bias_col):
    scores = jax.nn.sigmoid(logits_t)
    sel = scores + bias_col
    sc = [scores[e:e + 1, :] for e in range(N_EXPERTS)]
    sl = [sel[e:e + 1, :] for e in range(N_EXPERTS)]
    grp = []
    for g in range(N_GROUPS):
        v = sl[g * EXPERTS_PER_GROUP:(g + 1) * EXPERTS_PER_GROUP]
        best = None
        for a in range(EXPERTS_PER_GROUP):
            for b in range(a + 1, EXPERTS_PER_GROUP):
                pair = v[a] + v[b]
                best = pair if best is None else jnp.maximum(best, pair)
        grp.append(best)
    gates = []
    for g in range(N_GROUPS):
        is_best = None
        for g2 in range(N_GROUPS):
            if g2 == g:
                continue
            c = (grp[g] > grp[g2]) if g2 < g else (grp[g] >= grp[g2])
            is_best = c if is_best is None else jnp.logical_and(is_best, c)
        v = sl[g * EXPERTS_PER_GROUP:(g + 1) * EXPERTS_PER_GROUP]
        w = sc[g * EXPERTS_PER_GROUP:(g + 1) * EXPERTS_PER_GROUP]
        picked = []
        for a in range(EXPERTS_PER_GROUP):
            rank = jnp.zeros_like(v[a])
            for b in range(EXPERTS_PER_GROUP):
                if b == a:
                    continue
                ahead = (v[b] >= v[a]) if b < a else (v[b] > v[a])
                rank = rank + jnp.where(ahead, 1.0, 0.0)
            picked.append(jnp.where(jnp.logical_and(is_best, rank < 2.0), w[a], 0.0))
        gates.extend(picked)
    total = gates[0]
    for e in range(1, N_EXPERTS):
        total = total + gates[e]
    inv = 1.0 / total
    return [gt * inv for gt in gates]


def _out_moe_kernel(x_ref, yc_ref, ym_ref, yl_ref, mod_ref, gho_ref, e_ref, et_ref, wout_ref,
                    gffn_ref, wr_ref, br_ref, wgu_ref, wd_ref, o_ref, h2_scr, gate_scr, gt_scr):
    g = pl.program_id(1)
    ga2 = mod_ref[0:1, 5 * D_MODEL:6 * D_MODEL]

    @pl.when(g == 0)
    def _():
        ga1 = mod_ref[0:1, 2 * D_MODEL:3 * D_MODEL]
        sh2 = mod_ref[0:1, 3 * D_MODEL:4 * D_MODEL]
        sc2 = mod_ref[0:1, 4 * D_MODEL:5 * D_MODEL]
        y = jnp.concatenate([yc_ref[...], ym_ref[...], yl_ref[...]], axis=-1).astype(F32)
        ss = jnp.dot((y * y).astype(BF16), e_ref[...], preferred_element_type=F32)
        rinv = lax.rsqrt(ss * (1.0 / HEAD_DIM) + EPS)
        r_hi = rinv.astype(BF16)
        r_lo = (rinv - r_hi.astype(F32)).astype(BF16)
        rb = (jnp.dot(r_hi, et_ref[...], preferred_element_type=F32)
              + jnp.dot(r_lo, et_ref[...], preferred_element_type=F32))
        yn = y * rb * gho_ref[...]
        x1 = x_ref[...] + ga1 * jnp.dot(yn.astype(BF16), wout_ref[...], preferred_element_type=F32)
        o_ref[...] = x1
        ms = jnp.mean(x1 * x1, axis=-1, keepdims=True)
        h2 = x1 * lax.rsqrt(ms + EPS) * gffn_ref[...]
        h2 = h2 * (1.0 + sc2) + sh2
        h2_hi = h2.astype(BF16)
        h2_scr[...] = h2_hi
        h2_lo = (h2 - h2_hi.astype(F32)).astype(BF16)
        logits = (jnp.dot(h2_hi, wr_ref[0], preferred_element_type=F32)
                  + jnp.dot(h2_lo, wr_ref[0], preferred_element_type=F32)
                  + jnp.dot(h2_hi, wr_ref[1], preferred_element_type=F32))
        gates = _route(logits.T[0:N_EXPERTS, :], br_ref[0:N_EXPERTS, :])
        gt_scr[...] = jnp.zeros_like(gt_scr)
        for gg in range(N_GROUPS):
            for e in range(EXPERTS_PER_GROUP):
                gt_scr[gg, e:e + 1, :] = gates[gg * EXPERTS_PER_GROUP + e]
            gate_scr[gg] = gt_scr[gg].T

    h2b = h2_scr[...]
    gates_g = gate_scr[g]
    acts = []
    for e in range(EXPERTS_PER_GROUP):
        au = jnp.dot(h2b, wgu_ref[e], preferred_element_type=F32)
        a = au[:, :D_FF]
        u = au[:, D_FF:]
        act = (a * jax.nn.sigmoid(a)) * u * gates_g[:, e:e + 1]
        acts.append(act.astype(BF16))
    act = jnp.concatenate(acts, axis=-1)
    o_ref[...] += ga2 * jnp.dot(act, wd_ref[...], preferred_element_type=F32)


def _out_moe_call(l, x2, yc, ym, yl, mod, g_head_out, e_mat, et_mat, w_out, g_ffn, w_router_pad,
                  b_router_col, wgu, wd):
    S = x2.shape[0]
    tm = min(TM_OUT, S)
    row = lambda i, g: (i, 0)
    lay = lambda i, g: (l, 0, 0)
    fix = lambda i, g: (0, 0)
    epg = EXPERTS_PER_GROUP
    return pl.pallas_call(
        _out_moe_kernel,
        grid=(S // tm, N_GROUPS),
        in_specs=[
            pl.BlockSpec((tm, D_MODEL), row),
            pl.BlockSpec((tm, CONV_CH), row),
            pl.BlockSpec((tm, MLA_HEADS * V_HEAD), row),
            pl.BlockSpec((tm, LRU_W), row),
            pl.BlockSpec((None, SUBLANES, 6 * D_MODEL), lay),
            pl.BlockSpec((None, 1, D_MODEL), lay),
            pl.BlockSpec((D_MODEL, LANES), fix),
            pl.BlockSpec((LANES, D_MODEL), fix),
            pl.BlockSpec((None, D_MODEL, D_MODEL), lay),
            pl.BlockSpec((None, 1, D_MODEL), lay),
            pl.BlockSpec((2, D_MODEL, LANES), lambda i, g: (0, 0, 0)),
            pl.BlockSpec((LANES, 1), fix),
            pl.BlockSpec((None, epg, D_MODEL, 2 * D_FF), lambda i, g: (l, g, 0, 0)),
            pl.BlockSpec((None, epg * D_FF, D_MODEL), lambda i, g: (l, g, 0)),
        ],
        out_specs=pl.BlockSpec((tm, D_MODEL), row),
        out_shape=jax.ShapeDtypeStruct((S, D_MODEL), F32),
        scratch_shapes=[
            pltpu.VMEM((tm, D_MODEL), BF16),
            pltpu.VMEM((N_GROUPS, tm, LANES), F32),
            pltpu.VMEM((N_GROUPS, LANES, tm), F32),
        ],
        compiler_params=_cparams(("arbitrary", "arbitrary")),
        name="out_proj_moe",
    )(x2, yc, ym, yl, mod, g_head_out, e_mat, et_mat, w_out, g_ffn, w_router_pad, b_router_col,
      wgu, wd)


def _take_cols(w, idx, n_src):
    pad = jnp.zeros(w.shape[:-1] + (1,), w.dtype)
    return jnp.take(jnp.concatenate([w, pad], axis=-1), jnp.asarray(idx, jnp.int32), axis=-1)


def _in_proj_cols():
    zero = IN_COLS
    half = QK_ROPE // 2
    kro = 3 * CONV_CH + Q_LORA + KV_LORA
    lru0 = kro + QK_ROPE
    idx = list(range(0, 3 * CONV_CH))
    idx += list(range(lru0, lru0 + 2 * LRU_W))
    idx += list(range(3 * CONV_CH, kro))
    idx += [zero] * QK_NOPE + list(range(kro, kro + QK_ROPE)) + [zero] * (HP - QK_HEAD)
    idx += ([zero] * QK_NOPE + list(range(kro + half, kro + QK_ROPE)) + list(range(kro, kro + half))
            + [zero] * (HP - QK_HEAD))
    assert len(idx) == IN_EXT_COLS
    return idx


def _q_up_cols():
    zero = MLA_HEADS * QK_HEAD
    half = QK_ROPE // 2
    idx = []
    for h in range(MLA_HEADS):
        idx += list(range(h * QK_HEAD, (h + 1) * QK_HEAD)) + [zero] * (HP - QK_HEAD)
    for h in range(MLA_HEADS):
        r0 = h * QK_HEAD + QK_NOPE
        idx += ([zero] * QK_NOPE + list(range(r0 + half, r0 + QK_ROPE)) + list(range(r0, r0 + half))
                + [zero] * (HP - QK_HEAD))
    return idx


def _kv_up_cols():
    zero = MLA_HEADS * (QK_NOPE + V_HEAD)
    per = QK_NOPE + V_HEAD
    idx = []
    for h in range(MLA_HEADS):
        idx += list(range(h * per, h * per + QK_NOPE)) + [zero] * (HP - QK_NOPE)
    for h in range(MLA_HEADS):
        vcols = list(range(h * per + QK_NOPE, (h + 1) * per))
        idx += (vcols + [zero] * (HP - V_HEAD)) if h % 2 == 0 else ([zero] * (HP - V_HEAD) + vcols)
    return idx


def _ones_row():
    o = np.zeros((1, MLA_HEADS * HP), np.float32)
    for h in range(MLA_HEADS):
        o[0, h * HP + (V_HEAD if h % 2 == 0 else 0)] = 1.0
    return jnp.asarray(o)


def _rope_gain_rows(g):
    L = g.shape[0]
    half = QK_ROPE // 2
    z_tail = jnp.zeros((L, HP - QK_HEAD), g.dtype)
    z_nope = jnp.zeros((L, QK_NOPE), g.dtype)
    g1 = jnp.concatenate([g, z_tail], axis=-1)
    g2 = jnp.concatenate([z_nope, g[:, QK_NOPE + half:], g[:, QK_NOPE:QK_NOPE + half], z_tail], axis=-1)
    return jnp.stack([g1, g2], axis=1)


def _block_diag(w):
    L, nb, bw, _ = w.shape
    eye = jnp.eye(nb, dtype=w.dtype)
    return jnp.einsum("lncd,nm->lncmd", w, eye).reshape(L, nb * bw, nb * bw)


def kernel(x, c, positions, w_mod, b_mod, g_mix_norm, g_ffn_norm, w_in, conv_w, g_q_lora, g_kv_lora, w_q_up, w_kv_up, g_q_head, g_k_head, lru_conv_w, lru_conv_b, w_rgate, b_rgate, w_igate, b_igate, lru_lambda, g_head_out, w_out, w_router, b_router, w_exp_gate, w_exp_up, w_exp_down):
    B, S, D = x.shape
    assert B == 1 and D == D_MODEL
    L = w_mod.shape[0]

    win_ext = _take_cols(w_in, _in_proj_cols(), IN_COLS).astype(BF16)
    wq_ext = _take_cols(w_q_up, _q_up_cols(), MLA_HEADS * QK_HEAD).astype(BF16)
    wkv_ext = _take_cols(w_kv_up, _kv_up_cols(), MLA_HEADS * (QK_NOPE + V_HEAD)).astype(BF16)
    rope_gain = jnp.concatenate([_rope_gain_rows(g_q_head), _rope_gain_rows(g_k_head)], axis=1)
    ones_row = _ones_row()
    wr_bd = _block_diag(w_rgate).astype(BF16)
    wi_bd = _block_diag(w_igate).astype(BF16)
    r3 = lambda a: a.reshape(L, 1, a.shape[-1])
    head_of_col = np.arange(D_MODEL) // HEAD_DIM
    e_np = (head_of_col[:, None] == np.arange(LANES)[None, :]).astype(np.float32)
    e_mat = jnp.asarray(e_np, BF16)
    et_mat = jnp.asarray(e_np.T, BF16)
    w_out_b = w_out.astype(BF16)
    w_router_pad = jnp.pad(w_router, ((0, 0), (0, LANES - N_EXPERTS)))
    w_router_hi = w_router_pad.astype(BF16)
    w_router_lo = (w_router_pad - w_router_hi.astype(F32)).astype(BF16)
    w_router_pad = jnp.stack([w_router_hi, w_router_lo])
    b_router_col = jnp.pad(b_router, (0, LANES - N_EXPERTS)).reshape(LANES, 1)
    wgu = jnp.concatenate([w_exp_gate, w_exp_up], axis=-1).astype(BF16)
    wd = w_exp_down.astype(BF16).reshape(L, N_EXPERTS * D_FF, D_MODEL)

    mod = _mod_call(c, w_mod, b_mod)
    cs, sn = _rope_call(positions)

    x2 = x.reshape(S, D)
    for l in range(L):
        zc, q, k, v = _in_call(l, x2, mod, r3(g_mix_norm), win_ext, r3(g_q_lora), r3(g_kv_lora),
                               wq_ext, wkv_ext, rope_gain, ones_row, cs, sn)
        yc, yl = _seq_call(l, zc, conv_w, lru_conv_w, r3(lru_conv_b), wr_bd, r3(b_rgate), wi_bd,
                           r3(b_igate), r3(lru_lambda))
        ym = _attn_call(q, k, v)
        x2 = _out_moe_call(l, x2, yc, ym, yl, mod, r3(g_head_out), e_mat, et_mat, w_out_b,
                           r3(g_ffn_norm), w_router_pad, b_router_col, wgu, wd)
    return x2.reshape(B, S, D)
```
